```python
import math
import jax, jax.numpy as jnp
from jax import lax
import numpy as np

D_MODEL = 4096
BATCH = 4
SEQ = 2048
DEPTH = 1
DEC_BATCH = 128
DEC_SEQ = 4
PAST_LEN = 2048
PAGE_SIZE = 128

D_MIX = D_MODEL
D_ATT = D_MIX // 2
D_CONV = D_MIX - D_ATT
N_HEADS = 16
HEAD_V = D_ATT // N_HEADS
HEAD_QK = HEAD_V // 2
ROT_DIM = HEAD_QK // 4
ROPE_THETA = 500000.0
CONV_W = 3
Q_BLOCK = 128
EPS = 1e-6
SUBLN_EPS = 1e-5
SPLIT_WIDTHS = (D_ATT, D_ATT, D_ATT, D_ATT, D_CONV, D_CONV, D_CONV, D_CONV)
SPLIT_POINTS = tuple(int(s) for s in np.cumsum(SPLIT_WIDTHS)[:-1])
D_IN = int(sum(SPLIT_WIDTHS))

kernel_name = "hybrid_diffattn_shortconv_step"


def rmsnorm(x, w, eps):
    xf = x.astype(jnp.float32)
    y = xf * lax.rsqrt(jnp.mean(xf * xf, axis=-1, keepdims=True) + eps)
    return (y * w.astype(jnp.float32)).astype(x.dtype)


def partial_rope(x, pos):
    inv = ROPE_THETA ** (-jnp.arange(0, ROT_DIM, 2, dtype=jnp.float32) / ROT_DIM)
    ang = pos.astype(jnp.float32)[:, None] * inv[None, :]
    cos = jnp.cos(ang)[:, None, None, :]
    sin = jnp.sin(ang)[:, None, None, :]
    xr = x[..., :ROT_DIM].astype(jnp.float32)
    x1, x2 = xr[..., :ROT_DIM // 2], xr[..., ROT_DIM // 2:]
    rot = jnp.concatenate([x1 * cos - x2 * sin, x2 * cos + x1 * sin], axis=-1).astype(x.dtype)
    return jnp.concatenate([rot, x[..., ROT_DIM:]], axis=-1)


def mixer_inputs(x, norm_w, w_in, pos):
    b, t = x.shape[0], x.shape[1]
    h = rmsnorm(x, norm_w, EPS)
    z = jnp.einsum('btd,de->bte', h, w_in)
    q, k, v, g_att, gate_b, gate_c, hc, g_conv = jnp.split(z, SPLIT_POINTS, axis=-1)
    q = partial_rope(q.reshape(b, t, N_HEADS, 2, HEAD_QK), pos)
    k = partial_rope(k.reshape(b, t, N_HEADS, 2, HEAD_QK), pos)
    v = v.reshape(b, t, N_HEADS, HEAD_V)
    u = gate_c * hc
    return q, k, v, g_att, gate_b, u, g_conv


def diff_lambda(lq1, lk1, lq2, lk2, lam_init):
    f = jnp.float32
    return (jnp.exp(jnp.sum(lq1.astype(f) * lk1.astype(f)))
            - jnp.exp(jnp.sum(lq2.astype(f) * lk2.astype(f))) + lam_init)


def diff_attn_prompt(q, k, v, lam):
    b, t = q.shape[0], q.shape[1]
    nb = t // Q_BLOCK
    qb = jnp.moveaxis(q.reshape(b, nb, Q_BLOCK, N_HEADS, 2, HEAD_QK), 1, 0)
    kpos = jnp.arange(t)
    scale = HEAD_QK ** -0.5

    def block(args):
        qi, i = args
        s = jnp.einsum('bqhcd,bkhcd->bhcqk', qi, k).astype(jnp.float32) * scale
        qpos = i * Q_BLOCK + jnp.arange(Q_BLOCK)
        s = jnp.where(qpos[:, None] >= kpos[None, :], s, -jnp.inf)
        p = jax.nn.softmax(s, axis=-1)
        pd = (p[:, :, 0] - lam * p[:, :, 1]).astype(v.dtype)
        return jnp.einsum('bhqk,bkhe->bqhe', pd, v)

    o = lax.map(block, (qb, jnp.arange(nb)))
    return jnp.moveaxis(o, 0, 1).reshape(b, t, N_HEADS, HEAD_V)


def diff_attn_sample(q, k_new, v_new, k_past, v_past, lam):
    s_len = q.shape[1]
    p_len = k_past.shape[1]
    scale = HEAD_QK ** -0.5
    sp = jnp.einsum('bqhcd,bkhcd->bhcqk', q, k_past).astype(jnp.float32) * scale
    sn = jnp.einsum('bqhcd,bkhcd->bhcqk', q, k_new).astype(jnp.float32) * scale
    causal = jnp.arange(s_len)[:, None] >= jnp.arange(s_len)[None, :]
    sn = jnp.where(causal, sn, -jnp.inf)
    p = jax.nn.softmax(jnp.concatenate([sp, sn], axis=-1), axis=-1)
    pd = (p[:, :, 0] - lam * p[:, :, 1]).astype(v_new.dtype)
    return (jnp.einsum('bhqk,bkhe->bqhe', pd[..., :p_len], v_past)
            + jnp.einsum('bhqk,bkhe->bqhe', pd[..., p_len:], v_new))


def short_conv(u, prefix, conv_w):
    t = u.shape[1]
    up = jnp.concatenate([prefix, u], axis=1)
    y = conv_w[0] * up[:, 0:t]
    for j in range(1, CONV_W):
        y = y + conv_w[j] * up[:, j:j + t]
    return y, up[:, -(CONV_W - 1):]


def mixer_out(x, o_att, g_att, conv_y, gate_b, g_conv, subln_w, lam_init, w_out):
    b, t = x.shape[0], x.shape[1]
    o = rmsnorm(o_att, subln_w, SUBLN_EPS) * (1.0 - lam_init)
    o = o.reshape(b, t, D_ATT) * jax.nn.silu(g_att)
    c = gate_b * conv_y * jax.nn.silu(g_conv)
    return x + jnp.einsum('bte,ed->btd', jnp.concatenate([o, c], axis=-1), w_out)


def setup_inputs(seed: int = 0) -> dict:
    key = jax.random.key(seed)
    ks = jax.random.split(key, 16)
    n_pages = PAST_LEN // PAGE_SIZE
    n_used = DEC_BATCH * n_pages
    n_pool = n_used + n_used // 4
    f = jnp.float32
    x_prompt = jax.random.normal(ks[0], (BATCH, SEQ, D_MODEL), f)
    x_sample = jax.random.normal(ks[1], (DEC_BATCH, DEC_SEQ, D_MODEL), f)
    cache_k = jax.random.normal(ks[2], (DEPTH, n_pool, PAGE_SIZE, 2 * N_HEADS, HEAD_QK), f)
    cache_v = jax.random.normal(ks[3], (DEPTH, n_pool, PAGE_SIZE, N_HEADS, HEAD_V), f)
    state_conv = jax.random.normal(ks[4], (DEPTH, DEC_BATCH, CONV_W - 1, D_CONV), f)
    page_table = jax.random.permutation(ks[5], n_pool)[:n_used].reshape(DEC_BATCH, n_pages).astype(jnp.int32)
    norm_w = 1.0 + 0.01 * jax.random.normal(ks[6], (DEPTH, D_MODEL), f)
    w_in = jax.random.normal(ks[7], (DEPTH, D_MODEL, D_IN), f) * D_MODEL ** -0.5
    lambda_q1 = 0.1 * jax.random.normal(ks[8], (DEPTH, HEAD_QK), f)
    lambda_k1 = 0.1 * jax.random.normal(ks[9], (DEPTH, HEAD_QK), f)
    lambda_q2 = 0.1 * jax.random.normal(ks[10], (DEPTH, HEAD_QK), f)
    lambda_k2 = 0.1 * jax.random.normal(ks[11], (DEPTH, HEAD_QK), f)
    subln_w = 1.0 + 0.01 * jax.random.normal(ks[12], (DEPTH, HEAD_V), f)
    conv_w = jax.random.normal(ks[13], (DEPTH, CONV_W, D_CONV), f) * CONV_W ** -0.5
    w_out = jax.random.normal(ks[14], (DEPTH, D_MIX, D_MODEL), f) * D_MIX ** -0.5
    final_norm_w = 1.0 + 0.01 * jax.random.normal(ks[15], (D_MODEL,), f)
    return {"x_prompt": x_prompt, "x_sample": x_sample, "cache_k": cache_k, "cache_v": cache_v,
            "state_conv": state_conv, "page_table": page_table, "norm_w": norm_w, "w_in": w_in,
            "lambda_q1": lambda_q1, "lambda_k1": lambda_k1, "lambda_q2": lambda_q2,
            "lambda_k2": lambda_k2, "subln_w": subln_w, "conv_w": conv_w, "w_out": w_out,
            "final_norm_w": final_norm_w}


def reference(x_prompt, x_sample, cache_k, cache_v, state_conv, page_table, norm_w, w_in,
              lambda_q1, lambda_k1, lambda_q2, lambda_k2, subln_w, conv_w, w_out, final_norm_w):
    bp, t_p = x_prompt.shape[0], x_prompt.shape[1]
    bs, t_s = x_sample.shape[0], x_sample.shape[1]
    n_pages = PAST_LEN // PAGE_SIZE
    pos_p = jnp.arange(t_p)
    pos_s = PAST_LEN + jnp.arange(t_s)
    yp, ys = x_prompt, x_sample
    kp_l, vp_l, cp_l, ks_l, vs_l, cs_l = [], [], [], [], [], []
    for l in range(DEPTH):
        lam_init = 0.8 - 0.6 * math.exp(-0.3 * l)
        lam = diff_lambda(lambda_q1[l], lambda_k1[l], lambda_q2[l], lambda_k2[l], lam_init)

        q, k, v, g_att, gate_b, u, g_conv = mixer_inputs(yp, norm_w[l], w_in[l], pos_p)
        o_att = diff_attn_prompt(q, k, v, lam)
        conv_y, conv_st = short_conv(u, jnp.zeros((bp, CONV_W - 1, D_CONV), u.dtype), conv_w[l])
        yp = mixer_out(yp, o_att, g_att, conv_y, gate_b, g_conv, subln_w[l], lam_init, w_out[l])
        kp_l.append(k.reshape(bp, t_p, 2 * N_HEADS, HEAD_QK))
        vp_l.append(v)
        cp_l.append(conv_st)

        q, k, v, g_att, gate_b, u, g_conv = mixer_inputs(ys, norm_w[l], w_in[l], pos_s)
        k_past = cache_k[l][page_table].reshape(bs, n_pages * PAGE_SIZE, N_HEADS, 2, HEAD_QK)
        v_past = cache_v[l][page_table].reshape(bs, n_pages * PAGE_SIZE, N_HEADS, HEAD_V)
        o_att = diff_attn_sample(q, k, v, k_past, v_past, lam)
        conv_y, conv_st = short_conv(u, state_conv[l], conv_w[l])
        ys = mixer_out(ys, o_att, g_att, conv_y, gate_b, g_conv, subln_w[l], lam_init, w_out[l])
        ks_l.append(k.reshape(bs, t_s, 2 * N_HEADS, HEAD_QK))
        vs_l.append(v)
        cs_l.append(conv_st)

    y_prompt = rmsnorm(yp, final_norm_w, EPS)
    y_sample = rmsnorm(ys, final_norm_w, EPS)
    k_prompt = jnp.stack(kp_l)
    v_prompt = jnp.stack(vp_l)
    conv_prompt = jnp.stack(cp_l)
    k_sample = jnp.stack(ks_l)
    v_sample = jnp.stack(vs_l)
    conv_sample = jnp.stack(cs_l)
    return (y_prompt, y_sample, k_prompt, v_prompt, conv_prompt, k_sample, v_sample, conv_sample)
```

```python
import functools
import math

import numpy as np
import jax
import jax.numpy as jnp
from jax import lax
from jax.experimental import pallas as pl
from jax.experimental.pallas import tpu as pltpu

D_MODEL = 4096
PAST_LEN = 2048
PAGE_SIZE = 128
D_ATT = 2048
D_CONV = 2048
N_HEADS = 16
HEAD_V = 128
HEAD_QK = 64
ROT_DIM = 16
ROPE_THETA = 500000.0
CONV_W = 3
EPS = 1e-6
SUBLN_EPS = 1e-5
D_IN = 8 * 2048
LANES = 128
SUBLANES = 8
VMEM_LIMIT = 56 * 1024 * 1024
NEG_INF = float("-inf")

BLK_Q, BLK_K, BLK_V, BLK_GATT, BLK_B, BLK_C, BLK_H, BLK_GCONV = range(8)


def _params(sem):
    return pltpu.CompilerParams(dimension_semantics=sem, vmem_limit_bytes=VMEM_LIMIT)


def _rope_tables(pos):
    half = ROT_DIM // 2
    inv = ROPE_THETA ** (-jnp.arange(0, ROT_DIM, 2, dtype=jnp.float32) / ROT_DIM)
    ang = pos.astype(jnp.float32)[:, None] * inv[None, :]
    cos, sin = jnp.cos(ang), jnp.sin(ang)
    t = pos.shape[0]
    pad = jnp.zeros((t, HEAD_QK - ROT_DIM), jnp.float32)
    c = jnp.concatenate([cos, cos, pad + 1.0], axis=-1)
    s1 = jnp.concatenate([-sin, jnp.zeros_like(sin), pad], axis=-1)
    s2 = jnp.concatenate([jnp.zeros_like(sin), sin, pad], axis=-1)
    rep = LANES // HEAD_QK
    return jnp.tile(c, (1, rep)), jnp.tile(s1, (1, rep)), jnp.tile(s2, (1, rep))


def _inproj_kernel(x_ref, nw_ref, w_ref, c_ref, s1_ref, s2_ref, z_ref, h_ref, *, tn):
    j = pl.program_id(1)

    @pl.when(j == 0)
    def _():
        x = x_ref[...]
        ms = jnp.mean(x * x, axis=-1, keepdims=True)
        h_ref[...] = (x * lax.rsqrt(ms + EPS) * nw_ref[...]).astype(jnp.bfloat16)

    acc = jnp.dot(h_ref[...], w_ref[...], preferred_element_type=jnp.float32)
    n_rope = (2 * D_ATT) // tn
    n_q = D_ATT // tn

    @pl.when(j < n_rope)
    def _():
        scale = jnp.where(j < n_q, HEAD_QK ** -0.5, 1.0).astype(jnp.float32)
        c, s1, s2 = c_ref[...], s1_ref[...], s2_ref[...]
        for s in range(tn // LANES):
            a = acc[:, s * LANES:(s + 1) * LANES]
            r = a * c + pltpu.roll(a, LANES - ROT_DIM // 2, 1) * s1 + pltpu.roll(a, ROT_DIM // 2, 1) * s2
            z_ref[:, s * LANES:(s + 1) * LANES] = r * scale

    @pl.when(j >= n_rope)
    def _():
        z_ref[...] = acc


def _inproj(x2d, norm_w, w_bf16, tabs, *, tm, tn):
    m = x2d.shape[0]
    c, s1, s2 = tabs
    tab_spec = pl.BlockSpec((tm, LANES), lambda i, j: (i, 0))
    return pl.pallas_call(
        functools.partial(_inproj_kernel, tn=tn),
        grid=(m // tm, D_IN // tn),
        in_specs=[
            pl.BlockSpec((tm, D_MODEL), lambda i, j: (i, 0)),
            pl.BlockSpec((1, D_MODEL), lambda i, j: (0, 0)),
            pl.BlockSpec((D_MODEL, tn), lambda i, j: (0, j)),
            tab_spec, tab_spec, tab_spec,
        ],
        out_specs=pl.BlockSpec((tm, tn), lambda i, j: (i, j)),
        out_shape=jax.ShapeDtypeStruct((m, D_IN), jnp.float32),
        scratch_shapes=[pltpu.VMEM((tm, D_MODEL), jnp.bfloat16)],
        compiler_params=_params(("parallel", "arbitrary")),
        name="inproj",
    )(x2d, norm_w, w_bf16, c, s1, s2)


def _lambda_full(lq1, lk1, lq2, lk2, lam_init):
    a = jnp.exp(jnp.sum(lq1[...] * lk1[...], axis=-1, keepdims=True))
    b = jnp.exp(jnp.sum(lq2[...] * lk2[...], axis=-1, keepdims=True))
    return a - b + lam_init


def _prompt_attn_kernel(lq1, lk1, lq2, lk2, q_ref, k_ref, v_ref, o_ref, *, tq, lam_init):
    qi = pl.program_id(2)
    lam = _lambda_full(lq1, lk1, lq2, lk2, lam_init)
    q = q_ref[...]
    lane = lax.broadcasted_iota(jnp.int32, q.shape, 1)
    q0 = jnp.where(lane < HEAD_QK, q, 0.0)
    q1 = jnp.where(lane >= HEAD_QK, q, 0.0)
    qs = jnp.concatenate([q0, q1], axis=0).astype(jnp.bfloat16)

    def scores(kj):
        start = pl.multiple_of(kj * tq, tq)
        kb = k_ref[pl.ds(start, tq), :].astype(jnp.bfloat16)
        return lax.dot_general(qs, kb, (((1,), (1,)), ((), ())),
                               preferred_element_type=jnp.float32)

    def update(kj, s, carry):
        m, l, acc = carry
        start = pl.multiple_of(kj * tq, tq)
        vb = v_ref[pl.ds(start, tq), :].astype(jnp.bfloat16)
        m_new = jnp.maximum(m, jnp.max(s, axis=-1, keepdims=True))
        alpha = jnp.exp(m - m_new)
        p = jnp.exp(s - m_new)
        l = alpha * l + jnp.sum(p, axis=-1, keepdims=True)
        acc = alpha * acc + jnp.dot(p.astype(jnp.bfloat16), vb, preferred_element_type=jnp.float32)
        return m_new, l, acc

    def body(kj, carry):
        return update(kj, scores(kj), carry)

    init = (jnp.full((2 * tq, 1), NEG_INF, jnp.float32),
            jnp.zeros((2 * tq, 1), jnp.float32),
            jnp.zeros((2 * tq, HEAD_V), jnp.float32))
    carry = lax.fori_loop(0, qi, body, init)
    s = scores(qi)
    row = lax.broadcasted_iota(jnp.int32, s.shape, 0) % tq
    col = lax.broadcasted_iota(jnp.int32, s.shape, 1)
    s = jnp.where(row >= col, s, NEG_INF)
    m, l, acc = update(qi, s, carry)
    o = acc / l
    o_ref[...] = o[:tq] - lam * o[tq:]


def _prompt_attn(lams, z, batch, seq, *, tq, lam_init):
    nq = seq // tq
    lam_spec = pl.BlockSpec((1, HEAD_QK), lambda b, h, i: (0, 0))
    cpb = D_ATT // LANES
    return pl.pallas_call(
        functools.partial(_prompt_attn_kernel, tq=tq, lam_init=lam_init),
        grid=(batch, N_HEADS, nq),
        in_specs=[lam_spec] * 4 + [
            pl.BlockSpec((tq, LANES), lambda b, h, i: (b * nq + i, BLK_Q * cpb + h)),
            pl.BlockSpec((seq, LANES), lambda b, h, i: (b, BLK_K * cpb + h)),
            pl.BlockSpec((seq, LANES), lambda b, h, i: (b, BLK_V * cpb + h)),
        ],
        out_specs=pl.BlockSpec((tq, LANES), lambda b, h, i: (b * nq + i, h)),
        out_shape=jax.ShapeDtypeStruct((batch * seq, D_ATT), jnp.float32),
        compiler_params=_params(("parallel", "parallel", "arbitrary")),
        name="prompt_attn",
    )(*lams, z, z, z)


def _sample_attn_kernel(pt_ref, lq1, lk1, lq2, lk2, q_ref, kn_ref, vn_ref, *rest,
                        n_pages_step, t_s, lam_init):
    kp_refs = rest[:n_pages_step]
    vp_refs = rest[n_pages_step:2 * n_pages_step]
    o_ref, m_ref, l_ref, acc_ref = rest[2 * n_pages_step:]
    g = pl.program_id(1)
    rows = 2 * t_s
    lane = lax.broadcasted_iota(jnp.int32, (rows, LANES), 1)
    sub = lax.broadcasted_iota(jnp.int32, (rows, LANES), 0) // t_s
    tok = lax.broadcasted_iota(jnp.int32, (rows, 1), 0) % t_s
    own = (lane // HEAD_QK) == sub

    def qbd(h):
        qh = q_ref[0, :, h * LANES:(h + 1) * LANES]
        return jnp.where(own, jnp.concatenate([qh, qh], axis=0), 0.0)

    @pl.when(g == 0)
    def _():
        for h in range(N_HEADS):
            qh = qbd(h)
            kn = kn_ref[0, :, h * LANES:(h + 1) * LANES]
            vn = vn_ref[0, :, h * LANES:(h + 1) * LANES]
            sj = []
            for jn in range(t_s):
                s = jnp.sum(qh * kn[jn:jn + 1, :], axis=-1, keepdims=True)
                sj.append(jnp.where(tok >= jn, s, NEG_INF))
            m = functools.reduce(jnp.maximum, sj)
            l = jnp.zeros_like(m)
            acc = jnp.zeros((rows, LANES), jnp.float32)
            for jn in range(t_s):
                p = jnp.exp(sj[jn] - m)
                l = l + p
                acc = acc + p * vn[jn:jn + 1, :]
            m_ref[h] = jnp.broadcast_to(m, (rows, LANES))
            l_ref[h] = jnp.broadcast_to(l, (rows, LANES))
            acc_ref[h] = acc

    for h in range(N_HEADS):
        qh = qbd(h).astype(jnp.bfloat16)
        ss = []
        for kp in kp_refs:
            kt = kp[0, 2 * h:2 * h + 2, :, :].reshape(2 * HEAD_QK, PAGE_SIZE).astype(jnp.bfloat16)
            ss.append(jnp.dot(qh, kt, preferred_element_type=jnp.float32))
        m_old = m_ref[h]
        smax = functools.reduce(jnp.maximum, ss)
        m_new = jnp.maximum(m_old, jnp.max(smax, axis=-1, keepdims=True))
        alpha = jnp.exp(m_old - m_new)
        ps = [jnp.exp(s - m_new) for s in ss]
        psum = functools.reduce(jnp.add, ps)
        l_ref[h] = alpha * l_ref[h] + jnp.sum(psum, axis=-1, keepdims=True)
        pv = None
        for p, vp in zip(ps, vp_refs):
            vh = vp[0, pl.ds(h, PAGE_SIZE, stride=N_HEADS), :].astype(jnp.bfloat16)
            d = jnp.dot(p.astype(jnp.bfloat16), vh, preferred_element_type=jnp.float32)
            pv = d if pv is None else pv + d
        acc_ref[h] = alpha * acc_ref[h] + pv
        m_ref[h] = m_new

    @pl.when(g == pl.num_programs(1) - 1)
    def _():
        lam = _lambda_full(lq1, lk1, lq2, lk2, lam_init)
        for h in range(N_HEADS):
            o = acc_ref[h] / l_ref[h]
            o_ref[0, :, h * LANES:(h + 1) * LANES] = o[:t_s] - lam * o[t_s:]


def _sample_attn(lams, page_table, z3, cache_kt, cache_v2, *, n_pages_step, lam_init):
    bs, t_s = z3.shape[0], z3.shape[1]
    n_pages = page_table.shape[1]
    n_steps = n_pages // n_pages_step
    lam_spec = pl.BlockSpec((1, HEAD_QK), lambda b, g, pt: (0, 0))

    def zspec(blk):
        return pl.BlockSpec((1, t_s, D_ATT), lambda b, g, pt: (b, 0, blk))

    def kspec(p):
        return pl.BlockSpec((1, 2 * N_HEADS, HEAD_QK, PAGE_SIZE),
                            lambda b, g, pt: (pt[b, g * n_pages_step + p], 0, 0, 0))

    def vspec(p):
        return pl.BlockSpec((1, PAGE_SIZE * N_HEADS, HEAD_V),
                            lambda b, g, pt: (pt[b, g * n_pages_step + p], 0, 0))

    state = pltpu.VMEM((N_HEADS, 2 * t_s, LANES), jnp.float32)
    grid_spec = pltpu.PrefetchScalarGridSpec(
        num_scalar_prefetch=1,
        grid=(bs, n_steps),
        in_specs=[lam_spec] * 4 + [zspec(BLK_Q), zspec(BLK_K), zspec(BLK_V)]
        + [kspec(p) for p in range(n_pages_step)] + [vspec(p) for p in range(n_pages_step)],
        out_specs=pl.BlockSpec((1, t_s, D_ATT), lambda b, g, pt: (b, 0, 0)),
        scratch_shapes=[state, state, state],
    )
    return pl.pallas_call(
        functools.partial(_sample_attn_kernel, n_pages_step=n_pages_step, t_s=t_s, lam_init=lam_init),
        grid_spec=grid_spec,
        out_shape=jax.ShapeDtypeStruct((bs, t_s, D_ATT), jnp.float32),
        compiler_params=_params(("parallel", "arbitrary")),
        name="sample_attn",
    )(page_table, *lams, z3, z3, z3, *([cache_kt] * n_pages_step), *([cache_v2] * n_pages_step))


def _silu(x):
    return x * (1.0 / (1.0 + jnp.exp(-x)))


def _att_half(o, g_att, subln_w, lam_init):
    outs = []
    for h in range(N_HEADS):
        oh = o[:, h * LANES:(h + 1) * LANES]
        ms = jnp.mean(oh * oh, axis=-1, keepdims=True)
        on = (oh * lax.rsqrt(ms + SUBLN_EPS) * subln_w) * (1.0 - lam_init)
        outs.append(on * _silu(g_att[:, h * LANES:(h + 1) * LANES]))
    return jnp.concatenate(outs, axis=-1)


def _mix_prompt_kernel(o_ref, ga_ref, b_ref, c_ref, h_ref, gc_ref, cp_ref, hp_ref,
                       sw_ref, cw_ref, mix_ref, tail_ref, *, tm, seq, lam_init):
    i = pl.program_id(0)
    mix_ref[:, :D_ATT] = _att_half(o_ref[...], ga_ref[...], sw_ref[...], lam_init).astype(mix_ref.dtype)
    u = c_ref[...] * h_ref[...]
    first = (i * tm) % seq == 0
    uprev = jnp.where(first, 0.0, cp_ref[...] * hp_ref[...])
    up = jnp.concatenate([uprev, u], axis=0)
    cw = cw_ref[...]
    y = cw[0:1] * up[SUBLANES - 2:SUBLANES - 2 + tm] + cw[1:2] * up[SUBLANES - 1:SUBLANES - 1 + tm] + cw[2:3] * u
    mix_ref[:, D_ATT:] = (b_ref[...] * y * _silu(gc_ref[...])).astype(mix_ref.dtype)
    tail_ref[...] = u[tm - SUBLANES:]


def _mix_prompt(o, z, subln_w, conv_w, *, tm, seq, lam_init):
    m = z.shape[0]
    r8 = tm // SUBLANES

    def zspec(blk):
        return pl.BlockSpec((tm, D_ATT), lambda i: (i, blk))

    def prevspec(blk):
        return pl.BlockSpec((SUBLANES, D_ATT), lambda i: (jnp.maximum(i * r8 - 1, 0), blk))

    return pl.pallas_call(
        functools.partial(_mix_prompt_kernel, tm=tm, seq=seq, lam_init=lam_init),
        grid=(m // tm,),
        in_specs=[pl.BlockSpec((tm, D_ATT), lambda i: (i, 0)),
                  zspec(BLK_GATT), zspec(BLK_B), zspec(BLK_C), zspec(BLK_H), zspec(BLK_GCONV),
                  prevspec(BLK_C), prevspec(BLK_H),
                  pl.BlockSpec((1, HEAD_V), lambda i: (0, 0)),
                  pl.BlockSpec((CONV_W, D_CONV), lambda i: (0, 0))],
        out_specs=[pl.BlockSpec((tm, 2 * D_ATT), lambda i: (i, 0)),
                   pl.BlockSpec((SUBLANES, D_CONV), lambda i: (i, 0))],
        out_shape=[jax.ShapeDtypeStruct((m, 2 * D_ATT), jnp.bfloat16),
                   jax.ShapeDtypeStruct((m // tm * SUBLANES, D_CONV), jnp.float32)],
        compiler_params=_params(("arbitrary",)),
        name="mix_prompt",
    )(o, z, z, z, z, z, z, z, subln_w, conv_w)


def _mix_sample_kernel(o_ref, ga_ref, b_ref, c_ref, h_ref, gc_ref, st_ref,
                       sw_ref, cw_ref, mix_ref, nst_ref, *, t_s, lam_init):
    cw = cw_ref[...]
    ups = [st_ref[:, jn, :] for jn in range(CONV_W - 1)]
    ups += [c_ref[:, t, :] * h_ref[:, t, :] for t in range(t_s)]
    for t in range(t_s):
        y = cw[0:1] * ups[t]
        for jn in range(1, CONV_W):
            y = y + cw[jn:jn + 1] * ups[t + jn]
        cv = b_ref[:, t, :] * y * _silu(gc_ref[:, t, :])
        at = _att_half(o_ref[:, t, :], ga_ref[:, t, :], sw_ref[...], lam_init)
        mix_ref[:, t, :D_ATT] = at.astype(mix_ref.dtype)
        mix_ref[:, t, D_ATT:] = cv.astype(mix_ref.dtype)
    for jn in range(CONV_W - 1):
        nst_ref[:, jn, :] = ups[t_s + jn]


def _mix_sample(o3, z3, state, subln_w, conv_w, *, tb, lam_init):
    bs, t_s = z3.shape[0], z3.shape[1]

    def zspec(blk):
        return pl.BlockSpec((tb, t_s, D_ATT), lambda i: (i, 0, blk))

    st_spec = pl.BlockSpec((tb, CONV_W - 1, D_CONV), lambda i: (i, 0, 0))
    return pl.pallas_call(
        functools.partial(_mix_sample_kernel, t_s=t_s, lam_init=lam_init),
        grid=(bs // tb,),
        in_specs=[pl.BlockSpec((tb, t_s, D_ATT), lambda i: (i, 0, 0)),
                  zspec(BLK_GATT), zspec(BLK_B), zspec(BLK_C), zspec(BLK_H), zspec(BLK_GCONV),
                  st_spec,
                  pl.BlockSpec((1, HEAD_V), lambda i: (0, 0)),
                  pl.BlockSpec((CONV_W, D_CONV), lambda i: (0, 0))],
        out_specs=[pl.BlockSpec((tb, t_s, 2 * D_ATT), lambda i: (i, 0, 0)), st_spec],
        out_shape=[jax.ShapeDtypeStruct((bs, t_s, 2 * D_ATT), jnp.bfloat16),
                   jax.ShapeDtypeStruct((bs, CONV_W - 1, D_CONV), jnp.float32)],
        compiler_params=_params(("arbitrary",)),
        name="mix_sample",
    )(o3, z3, z3, z3, z3, z3, state, subln_w, conv_w)


def _outproj_kernel(mix_ref, w_ref, x_ref, fw_ref, y_ref, *, tn, final_norm):
    j = pl.program_id(1)
    col = pl.multiple_of(j * tn, tn)
    y_ref[:, pl.ds(col, tn)] = x_ref[...] + jnp.dot(mix_ref[...], w_ref[...],
                                                    preferred_element_type=jnp.float32)

    if final_norm:
        @pl.when(j == pl.num_programs(1) - 1)
        def _():
            y = y_ref[...]
            ms = jnp.mean(y * y, axis=-1, keepdims=True)
            y_ref[...] = y * lax.rsqrt(ms + EPS) * fw_ref[...]


def _outproj(mix, w_bf16, x2d, final_w, *, tm, tn, final_norm):
    m = x2d.shape[0]
    return pl.pallas_call(
        functools.partial(_outproj_kernel, tn=tn, final_norm=final_norm),
        grid=(m // tm, D_MODEL // tn),
        in_specs=[pl.BlockSpec((tm, D_MODEL), lambda i, j: (i, 0)),
                  pl.BlockSpec((D_MODEL, tn), lambda i, j: (0, j)),
                  pl.BlockSpec((tm, tn), lambda i, j: (i, j)),
                  pl.BlockSpec((1, D_MODEL), lambda i, j: (0, 0))],
        out_specs=pl.BlockSpec((tm, D_MODEL), lambda i, j: (i, 0)),
        out_shape=jax.ShapeDtypeStruct((m, D_MODEL), jnp.float32),
        compiler_params=_params(("parallel", "arbitrary")),
        name="outproj",
    )(mix, w_bf16, x2d, final_w)


def kernel(x_prompt, x_sample, cache_k, cache_v, state_conv, page_table, norm_w, w_in,
           lambda_q1, lambda_k1, lambda_q2, lambda_k2, subln_w, conv_w, w_out, final_norm_w):
    bp, t_p = x_prompt.shape[0], x_prompt.shape[1]
    bs, t_s = x_sample.shape[0], x_sample.shape[1]
    depth = norm_w.shape[0]
    n_pool = cache_k.shape[1]
    xp = x_prompt.reshape(bp * t_p, D_MODEL)
    xs = x_sample.reshape(bs * t_s, D_MODEL)
    tabs_p = _rope_tables(jnp.arange(t_p))
    tabs_p = tuple(jnp.tile(t, (bp, 1)) for t in tabs_p)
    tabs_s = _rope_tables(PAST_LEN + jnp.arange(t_s))
    tabs_s = tuple(jnp.tile(t, (bs, 1)) for t in tabs_s)
    final_w = final_norm_w.reshape(1, D_MODEL)

    kp_l, vp_l, cp_l, ks_l, vs_l, cs_l = [], [], [], [], [], []
    for l in range(depth):
        lam_init = 0.8 - 0.6 * math.exp(-0.3 * l)
        lams = (lambda_q1[l:l + 1], lambda_k1[l:l + 1], lambda_q2[l:l + 1], lambda_k2[l:l + 1])
        w_in_b = w_in[l].astype(jnp.bfloat16)
        w_out_b = w_out[l].astype(jnp.bfloat16)
        nw = norm_w[l:l + 1]
        sw = subln_w[l:l + 1]
        cw = conv_w[l]

        zp = _inproj(xp, nw, w_in_b, tabs_p, tm=512, tn=1024)
        op = _prompt_attn(lams, zp, bp, t_p, tq=256, lam_init=lam_init)
        tm_mix = 256
        mixp, tails = _mix_prompt(op, zp, sw, cw, tm=tm_mix, seq=t_p, lam_init=lam_init)
        xp = _outproj(mixp, w_out_b, xp, final_w, tm=512, tn=1024, final_norm=(l == depth - 1))
        kp_l.append(zp[:, BLK_K * D_ATT:(BLK_K + 1) * D_ATT].reshape(bp, t_p, 2 * N_HEADS, HEAD_QK))
        vp_l.append(zp[:, BLK_V * D_ATT:(BLK_V + 1) * D_ATT].reshape(bp, t_p, N_HEADS, HEAD_V))
        tails = tails.reshape(bp, t_p // tm_mix, SUBLANES, D_CONV)
        cp_l.append(tails[:, -1, SUBLANES - (CONV_W - 1):, :])

        zs = _inproj(xs, nw, w_in_b, tabs_s, tm=512, tn=1024)
        zs3 = zs.reshape(bs, t_s, D_IN)
        cache_kt = jnp.transpose(cache_k[l], (0, 2, 3, 1))
        cache_v2 = cache_v[l].reshape(n_pool, PAGE_SIZE * N_HEADS, HEAD_V)
        os3 = _sample_attn(lams, page_table, zs3, cache_kt, cache_v2, n_pages_step=4, lam_init=lam_init)
        mixs, nst = _mix_sample(os3, zs3, state_conv[l], sw, cw, tb=8, lam_init=lam_init)
        xs = _outproj(mixs.reshape(bs * t_s, 2 * D_ATT), w_out_b, xs, final_w, tm=512, tn=1024,
                      final_norm=(l == depth - 1))
        ks_l.append(zs[:, BLK_K * D_ATT:(BLK_K + 1) * D_ATT].reshape(bs, t_s, 2 * N_HEADS, HEAD_QK))
        vs_l.append(zs[:, BLK_V * D_ATT:(BLK_V + 1) * D_ATT].reshape(bs, t_s, N_HEADS, HEAD_V))
        cs_l.append(nst)

    y_prompt = xp.reshape(bp, t_p, D_MODEL)
    y_sample = xs.reshape(bs, t_s, D_MODEL)
    return (y_prompt, y_sample, jnp.stack(kp_l), jnp.stack(vp_l), jnp.stack(cp_l),
            jnp.stack(ks_l), jnp.stack(vs_l), jnp.stack(cs_l))
```

```python
import functools
import math

import numpy as np
import jax
import jax.numpy as jnp
from jax import lax
from jax.experimental import pallas as pl
from jax.experimental.pallas import tpu as pltpu

D_MODEL = 4096
PAST_LEN = 2048
PAGE_SIZE = 128
D_ATT = 2048
D_CONV = 2048
N_HEADS = 16
HEAD_V = 128
HEAD_QK = 64
ROT_DIM = 16
ROPE_THETA = 500000.0
CONV_W = 3
EPS = 1e-6
SUBLN_EPS = 1e-5
D_IN = 8 * 2048
LANES = 128
SUBLANES = 8
VMEM_LIMIT = 56 * 1024 * 1024
NEG_INF = float("-inf")

BLK_Q, BLK_K, BLK_V, BLK_GATT, BLK_B, BLK_C, BLK_H, BLK_GCONV = range(8)


def _params(sem):
    return pltpu.CompilerParams(dimension_semantics=sem, vmem_limit_bytes=VMEM_LIMIT)


def _rope_tables(pos):
    half = ROT_DIM // 2
    inv = ROPE_THETA ** (-jnp.arange(0, ROT_DIM, 2, dtype=jnp.float32) / ROT_DIM)
    ang = pos.astype(jnp.float32)[:, None] * inv[None, :]
    cos, sin = jnp.cos(ang), jnp.sin(ang)
    t = pos.shape[0]
    pad = jnp.zeros((t, HEAD_QK - ROT_DIM), jnp.float32)
    c = jnp.concatenate([cos, cos, pad + 1.0], axis=-1)
    s1 = jnp.concatenate([-sin, jnp.zeros_like(sin), pad], axis=-1)
    s2 = jnp.concatenate([jnp.zeros_like(sin), sin, pad], axis=-1)
    rep = LANES // HEAD_QK
    return jnp.tile(c, (1, rep)), jnp.tile(s1, (1, rep)), jnp.tile(s2, (1, rep))


def _inproj_kernel(x_ref, nw_ref, w_ref, c_ref, s1_ref, s2_ref, z_ref, h_ref, *, tn):
    j = pl.program_id(1)

    @pl.when(j == 0)
    def _():
        x = x_ref[...]
        ms = jnp.mean(x * x, axis=-1, keepdims=True)
        h_ref[...] = (x * lax.rsqrt(ms + EPS) * nw_ref[...]).astype(jnp.bfloat16)

    acc = jnp.dot(h_ref[...], w_ref[...], preferred_element_type=jnp.float32)
    n_rope = (2 * D_ATT) // tn
    n_q = D_ATT // tn

    @pl.when(j < n_rope)
    def _():
        scale = jnp.where(j < n_q, HEAD_QK ** -0.5, 1.0).astype(jnp.float32)
        c, s1, s2 = c_ref[...], s1_ref[...], s2_ref[...]
        for s in range(tn // LANES):
            a = acc[:, s * LANES:(s + 1) * LANES]
            r = a * c + pltpu.roll(a, LANES - ROT_DIM // 2, 1) * s1 + pltpu.roll(a, ROT_DIM // 2, 1) * s2
            z_ref[:, s * LANES:(s + 1) * LANES] = r * scale

    @pl.when(j >= n_rope)
    def _():
        z_ref[...] = acc


def _inproj(x2d, norm_w, w_bf16, tabs, *, tm, tn):
    m = x2d.shape[0]
    c, s1, s2 = tabs
    tab_spec = pl.BlockSpec((tm, LANES), lambda i, j: (i, 0))
    return pl.pallas_call(
        functools.partial(_inproj_kernel, tn=tn),
        grid=(m // tm, D_IN // tn),
        in_specs=[
            pl.BlockSpec((tm, D_MODEL), lambda i, j: (i, 0)),
            pl.BlockSpec((1, D_MODEL), lambda i, j: (0, 0)),
            pl.BlockSpec((D_MODEL, tn), lambda i, j: (0, j)),
            tab_spec, tab_spec, tab_spec,
        ],
        out_specs=pl.BlockSpec((tm, tn), lambda i, j: (i, j)),
        out_shape=jax.ShapeDtypeStruct((m, D_IN), jnp.float32),
        scratch_shapes=[pltpu.VMEM((tm, D_MODEL), jnp.bfloat16)],
        compiler_params=_params(("parallel", "arbitrary")),
        name="inproj",
    )(x2d, norm_w, w_bf16, c, s1, s2)


def _lambda_full(lq1, lk1, lq2, lk2, lam_init):
    a = jnp.exp(jnp.sum(lq1[...] * lk1[...], axis=-1, keepdims=True))
    b = jnp.exp(jnp.sum(lq2[...] * lk2[...], axis=-1, keepdims=True))
    return a - b + lam_init


ATTN_ROW_CHUNK = 64


def _prompt_attn_kernel(lq1, lk1, lq2, lk2, q_ref, k_ref, v_ref, o_ref,
                        kb_ref, vb_ref, qs_ref, s_ref, p_ref, m_ref, l_ref, al_ref, acc_ref,
                        *, tq, lam_init):
    seq = q_ref.shape[0]
    nq = seq // tq
    kb_ref[...] = k_ref[...].astype(jnp.bfloat16)
    vb_ref[:, :HEAD_V] = v_ref[...].astype(jnp.bfloat16)
    vb_ref[:, HEAD_V:] = jnp.ones(v_ref.shape, jnp.bfloat16)
    lam = _lambda_full(lq1, lk1, lq2, lk2, lam_init)

    def setup(qi):
        par = qi % 2
        q = q_ref[qi * tq:(qi + 1) * tq, :]
        lane = lax.broadcasted_iota(jnp.int32, q.shape, 1)
        qs_ref[par, 0] = jnp.where(lane < HEAD_QK, q, 0.0).astype(jnp.bfloat16)
        qs_ref[par, 1] = jnp.where(lane >= HEAD_QK, q, 0.0).astype(jnp.bfloat16)
        m_ref[par] = jnp.full(m_ref.shape[1:], NEG_INF, jnp.float32)
        l_ref[par] = jnp.zeros(l_ref.shape[1:], jnp.float32)
        acc_ref[par] = jnp.zeros(acc_ref.shape[1:], jnp.float32)

    def scores(qi, x, kj):
        s_ref[qi % 2, x] = lax.dot_general(
            qs_ref[qi % 2, x], kb_ref[kj * tq:(kj + 1) * tq, :], (((1,), (1,)), ((), ())),
            preferred_element_type=jnp.float32)

    def softmax(qi, x, kj):
        par = qi % 2
        for c in range(tq // ATTN_ROW_CHUNK):
            rows = slice(c * ATTN_ROW_CHUNK, (c + 1) * ATTN_ROW_CHUNK)
            s = s_ref[par, x, rows, :]
            if kj == qi:
                row = c * ATTN_ROW_CHUNK + lax.broadcasted_iota(jnp.int32, s.shape, 0)
                col = lax.broadcasted_iota(jnp.int32, s.shape, 1)
                s = jnp.where(row >= col, s, NEG_INF)
            m_old = m_ref[par, x, rows, :]
            m_new = jnp.maximum(m_old, jnp.max(s, axis=-1, keepdims=True))
            al_ref[par, x, rows, :] = jnp.exp(m_old - m_new)
            m_ref[par, x, rows, :] = m_new
            p_ref[par, x, rows, :] = jnp.exp(s - m_new[:, :1]).astype(jnp.bfloat16)

    def pv(qi, x, kj):
        par = qi % 2
        r = jnp.dot(p_ref[par, x], vb_ref[kj * tq:(kj + 1) * tq, :],
                    preferred_element_type=jnp.float32)
        al = al_ref[par, x]
        acc_ref[par, x] = al * acc_ref[par, x] + r[:, :HEAD_V]
        l_ref[par, x] = al * l_ref[par, x] + r[:, HEAD_V:]

    def head(qi):
        setup(qi)
        scores(qi, 0, 0)
        scores(qi, 1, 0)
        softmax(qi, 0, 0)

    def middle(qi):
        for kj in range(qi):
            pv(qi, 0, kj)
            scores(qi, 0, kj + 1)
            softmax(qi, 1, kj)
            pv(qi, 1, kj)
            scores(qi, 1, kj + 1)
            softmax(qi, 0, kj + 1)

    def tail(qi):
        par = qi % 2
        pv(qi, 0, qi)
        softmax(qi, 1, qi)
        pv(qi, 1, qi)
        o_ref[qi * tq:(qi + 1) * tq, :] = (acc_ref[par, 0] / l_ref[par, 0]
                                          - lam * (acc_ref[par, 1] / l_ref[par, 1]))

    head(0)
    for qi in range(nq):
        middle(qi)
        if qi + 1 < nq:
            head(qi + 1)
        tail(qi)


def _prompt_attn(lams, z, batch, seq, *, tq, lam_init):
    lam_spec = pl.BlockSpec((1, HEAD_QK), lambda b, h: (0, 0))
    cpb = D_ATT // LANES
    return pl.pallas_call(
        functools.partial(_prompt_attn_kernel, tq=tq, lam_init=lam_init),
        grid=(batch, N_HEADS),
        in_specs=[lam_spec] * 4 + [
            pl.BlockSpec((seq, LANES), lambda b, h: (b, BLK_Q * cpb + h)),
            pl.BlockSpec((seq, LANES), lambda b, h: (b, BLK_K * cpb + h)),
            pl.BlockSpec((seq, LANES), lambda b, h: (b, BLK_V * cpb + h)),
        ],
        out_specs=pl.BlockSpec((seq, LANES), lambda b, h: (b, h)),
        out_shape=jax.ShapeDtypeStruct((batch * seq, D_ATT), jnp.float32),
        scratch_shapes=[pltpu.VMEM((seq, LANES), jnp.bfloat16),
                        pltpu.VMEM((seq, 2 * HEAD_V), jnp.bfloat16),
                        pltpu.VMEM((2, 2, tq, LANES), jnp.bfloat16),
                        pltpu.VMEM((2, 2, tq, tq), jnp.float32),
                        pltpu.VMEM((2, 2, tq, tq), jnp.bfloat16),
                        pltpu.VMEM((2, 2, tq, LANES), jnp.float32),
                        pltpu.VMEM((2, 2, tq, LANES), jnp.float32),
                        pltpu.VMEM((2, 2, tq, LANES), jnp.float32),
                        pltpu.VMEM((2, 2, tq, HEAD_V), jnp.float32)],
        compiler_params=_params(("parallel", "parallel")),
        name="prompt_attn",
    )(*lams, z, z, z)


def _sample_attn_kernel(pt_ref, lq1, lk1, lq2, lk2, q_ref, kn_ref, vn_ref, *rest,
                        n_pages_step, t_s, lam_init):
    kp_refs = rest[:n_pages_step]
    vp_refs = rest[n_pages_step:2 * n_pages_step]
    o_ref, m_ref, l_ref, acc_ref, vt_ref = rest[2 * n_pages_step:]
    g = pl.program_id(1)
    rows = 2 * t_s
    lane = lax.broadcasted_iota(jnp.int32, (rows, LANES), 1)
    sub = lax.broadcasted_iota(jnp.int32, (rows, LANES), 0) // t_s
    tok = lax.broadcasted_iota(jnp.int32, (rows, 1), 0) % t_s
    own = (lane // HEAD_QK) == sub

    def qbd(h):
        qh = q_ref[0, :, h * LANES:(h + 1) * LANES]
        return jnp.where(own, jnp.concatenate([qh, qh], axis=0), 0.0)

    @pl.when(g == 0)
    def _():
        for h in range(N_HEADS):
            qh = qbd(h)
            kn = kn_ref[0, :, h * LANES:(h + 1) * LANES]
            vn = vn_ref[0, :, h * LANES:(h + 1) * LANES]
            sj = []
            for jn in range(t_s):
                s = jnp.sum(qh * kn[jn:jn + 1, :], axis=-1, keepdims=True)
                sj.append(jnp.where(tok >= jn, s, NEG_INF))
            m = functools.reduce(jnp.maximum, sj)
            l = jnp.zeros_like(m)
            acc = jnp.zeros((rows, LANES), jnp.float32)
            for jn in range(t_s):
                p = jnp.exp(sj[jn] - m)
                l = l + p
                acc = acc + p * vn[jn:jn + 1, :]
            m_ref[h * rows:(h + 1) * rows] = jnp.broadcast_to(m, (rows, LANES))
            l_ref[h * rows:(h + 1) * rows] = jnp.broadcast_to(l, (rows, LANES))
            acc_ref[h * rows:(h + 1) * rows] = acc

    pos_chunk = 2 * SUBLANES
    for p, vp in enumerate(vp_refs):
        for c in range(PAGE_SIZE // pos_chunk):
            x = vp[0, pl.ds(c * pos_chunk * N_HEADS, pos_chunk * N_HEADS), :].astype(jnp.bfloat16)
            x = jnp.swapaxes(x.reshape(pos_chunk, N_HEADS, HEAD_V), 0, 1)
            vt_ref[:, pl.ds(p * PAGE_SIZE + c * pos_chunk, pos_chunk), :] = x

    ss = []
    for h in range(N_HEADS):
        kt = jnp.concatenate(
            [kp[0, 2 * h:2 * h + 2, :, :].reshape(2 * HEAD_QK, PAGE_SIZE) for kp in kp_refs],
            axis=1).astype(jnp.bfloat16)
        ss.append(jnp.dot(qbd(h).astype(jnp.bfloat16), kt, preferred_element_type=jnp.float32))
    s = jnp.concatenate(ss, axis=0)
    m_old = m_ref[...]
    m_new = jnp.maximum(m_old, jnp.max(s, axis=-1, keepdims=True))
    alpha = jnp.exp(m_old - m_new)
    p = jnp.exp(s - m_new[:, :1])
    l_ref[...] = alpha * l_ref[...] + jnp.sum(p, axis=-1, keepdims=True)
    m_ref[...] = m_new
    pv = [jnp.dot(p[h * rows:(h + 1) * rows].astype(jnp.bfloat16), vt_ref[h],
                  preferred_element_type=jnp.float32) for h in range(N_HEADS)]
    acc_ref[...] = alpha * acc_ref[...] + jnp.concatenate(pv, axis=0)

    @pl.when(g == pl.num_programs(1) - 1)
    def _():
        lam = _lambda_full(lq1, lk1, lq2, lk2, lam_init)
        o = acc_ref[...] / l_ref[...]
        for h in range(N_HEADS):
            oh = o[h * rows:(h + 1) * rows]
            o_ref[0, :, h * LANES:(h + 1) * LANES] = oh[:t_s] - lam * oh[t_s:]


def _sample_attn(lams, page_table, z3, cache_kt, cache_v2, *, n_pages_step, lam_init):
    bs, t_s = z3.shape[0], z3.shape[1]
    n_pages = page_table.shape[1]
    n_steps = n_pages // n_pages_step
    lam_spec = pl.BlockSpec((1, HEAD_QK), lambda b, g, pt: (0, 0))

    def zspec(blk):
        return pl.BlockSpec((1, t_s, D_ATT), lambda b, g, pt: (b, 0, blk))

    def kspec(p):
        return pl.BlockSpec((1, 2 * N_HEADS, HEAD_QK, PAGE_SIZE),
                            lambda b, g, pt: (pt[b, g * n_pages_step + p], 0, 0, 0))

    def vspec(p):
        return pl.BlockSpec((1, PAGE_SIZE * N_HEADS, HEAD_V),
                            lambda b, g, pt: (pt[b, g * n_pages_step + p], 0, 0))

    state = pltpu.VMEM((N_HEADS * 2 * t_s, LANES), jnp.float32)
    grid_spec = pltpu.PrefetchScalarGridSpec(
        num_scalar_prefetch=1,
        grid=(bs, n_steps),
        in_specs=[lam_spec] * 4 + [zspec(BLK_Q), zspec(BLK_K), zspec(BLK_V)]
        + [kspec(p) for p in range(n_pages_step)] + [vspec(p) for p in range(n_pages_step)],
        out_specs=pl.BlockSpec((1, t_s, D_ATT), lambda b, g, pt: (b, 0, 0)),
        scratch_shapes=[state, state, state,
                        pltpu.VMEM((N_HEADS, n_pages_step * PAGE_SIZE, HEAD_V), jnp.bfloat16)],
    )
    return pl.pallas_call(
        functools.partial(_sample_attn_kernel, n_pages_step=n_pages_step, t_s=t_s, lam_init=lam_init),
        grid_spec=grid_spec,
        out_shape=jax.ShapeDtypeStruct((bs, t_s, D_ATT), jnp.float32),
        compiler_params=_params(("parallel", "arbitrary")),
        name="sample_attn",
    )(page_table, *lams, z3, z3, z3, *([cache_kt] * n_pages_step), *([cache_v2] * n_pages_step))


def _silu(x):
    return x * (1.0 / (1.0 + jnp.exp(-x)))


def _att_half(o, g_att, subln_w, lam_init):
    outs = []
    for h in range(N_HEADS):
        oh = o[:, h * LANES:(h + 1) * LANES]
        ms = jnp.mean(oh * oh, axis=-1, keepdims=True)
        on = (oh * lax.rsqrt(ms + SUBLN_EPS) * subln_w) * (1.0 - lam_init)
        outs.append(on * _silu(g_att[:, h * LANES:(h + 1) * LANES]))
    return jnp.concatenate(outs, axis=-1)


def _mix_prompt_kernel(o_ref, ga_ref, b_ref, c_ref, h_ref, gc_ref, cp_ref, hp_ref,
                       sw_ref, cw_ref, mix_ref, tail_ref, *, tm, seq, lam_init):
    i = pl.program_id(0)
    mix_ref[:, :D_ATT] = _att_half(o_ref[...], ga_ref[...], sw_ref[...], lam_init).astype(mix_ref.dtype)
    u = c_ref[...] * h_ref[...]
    first = (i * tm) % seq == 0
    uprev = jnp.where(first, 0.0, cp_ref[...] * hp_ref[...])
    up = jnp.concatenate([uprev, u], axis=0)
    cw = cw_ref[...]
    y = cw[0:1] * up[SUBLANES - 2:SUBLANES - 2 + tm] + cw[1:2] * up[SUBLANES - 1:SUBLANES - 1 + tm] + cw[2:3] * u
    mix_ref[:, D_ATT:] = (b_ref[...] * y * _silu(gc_ref[...])).astype(mix_ref.dtype)
    tail_ref[...] = u[tm - SUBLANES:]


def _mix_prompt(o, z, subln_w, conv_w, *, tm, seq, lam_init):
    m = z.shape[0]
    r8 = tm // SUBLANES

    def zspec(blk):
        return pl.BlockSpec((tm, D_ATT), lambda i: (i, blk))

    def prevspec(blk):
        return pl.BlockSpec((SUBLANES, D_ATT), lambda i: (jnp.maximum(i * r8 - 1, 0), blk))

    return pl.pallas_call(
        functools.partial(_mix_prompt_kernel, tm=tm, seq=seq, lam_init=lam_init),
        grid=(m // tm,),
        in_specs=[pl.BlockSpec((tm, D_ATT), lambda i: (i, 0)),
                  zspec(BLK_GATT), zspec(BLK_B), zspec(BLK_C), zspec(BLK_H), zspec(BLK_GCONV),
                  prevspec(BLK_C), prevspec(BLK_H),
                  pl.BlockSpec((1, HEAD_V), lambda i: (0, 0)),
                  pl.BlockSpec((CONV_W, D_CONV), lambda i: (0, 0))],
        out_specs=[pl.BlockSpec((tm, 2 * D_ATT), lambda i: (i, 0)),
                   pl.BlockSpec((SUBLANES, D_CONV), lambda i: (i, 0))],
        out_shape=[jax.ShapeDtypeStruct((m, 2 * D_ATT), jnp.bfloat16),
                   jax.ShapeDtypeStruct((m // tm * SUBLANES, D_CONV), jnp.float32)],
        compiler_params=_params(("arbitrary",)),
        name="mix_prompt",
    )(o, z, z, z, z, z, z, z, subln_w, conv_w)


def _mix_sample_kernel(o_ref, ga_ref, b_ref, c_ref, h_ref, gc_ref, st_ref,
                       sw_ref, cw_ref, mix_ref, nst_ref, *, t_s, lam_init):
    cw = cw_ref[...]
    ups = [st_ref[:, jn, :] for jn in range(CONV_W - 1)]
    ups += [c_ref[:, t, :] * h_ref[:, t, :] for t in range(t_s)]
    for t in range(t_s):
        y = cw[0:1] * ups[t]
        for jn in range(1, CONV_W):
            y = y + cw[jn:jn + 1] * ups[t + jn]
        cv = b_ref[:, t, :] * y * _silu(gc_ref[:, t, :])
        at = _att_half(o_ref[:, t, :], ga_ref[:, t, :], sw_ref[...], lam_init)
        mix_ref[:, t, :D_ATT] = at.astype(mix_ref.dtype)
        mix_ref[:, t, D_ATT:] = cv.astype(mix_ref.dtype)
    for jn in range(CONV_W - 1):
        nst_ref[:, jn, :] = ups[t_s + jn]


def _mix_sample(o3, z3, state, subln_w, conv_w, *, tb, lam_init):
    bs, t_s = z3.shape[0], z3.shape[1]

    def zspec(blk):
        return pl.BlockSpec((tb, t_s, D_ATT), lambda i: (i, 0, blk))

    st_spec = pl.BlockSpec((tb, CONV_W - 1, D_CONV), lambda i: (i, 0, 0))
    return pl.pallas_call(
        functools.partial(_mix_sample_kernel, t_s=t_s, lam_init=lam_init),
        grid=(bs // tb,),
        in_specs=[pl.BlockSpec((tb, t_s, D_ATT), lambda i: (i, 0, 0)),
                  zspec(BLK_GATT), zspec(BLK_B), zspec(BLK_C), zspec(BLK_H), zspec(BLK_GCONV),
                  st_spec,
                  pl.BlockSpec((1, HEAD_V), lambda i: (0, 0)),
                  pl.BlockSpec((CONV_W, D_CONV), lambda i: (0, 0))],
        out_specs=[pl.BlockSpec((tb, t_s, 2 * D_ATT), lambda i: (i, 0, 0)), st_spec],
        out_shape=[jax.ShapeDtypeStruct((bs, t_s, 2 * D_ATT), jnp.bfloat16),
                   jax.ShapeDtypeStruct((bs, CONV_W - 1, D_CONV), jnp.float32)],
        compiler_params=_params(("arbitrary",)),
        name="mix_sample",
    )(o3, z3, z3, z3, z3, z3, state, subln_w, conv_w)


def _outproj_kernel(mix_ref, w_ref, x_ref, fw_ref, y_ref, *, tn, final_norm):
    j = pl.program_id(1)
    col = pl.multiple_of(j * tn, tn)
    y_ref[:, pl.ds(col, tn)] = x_ref[...] + jnp.dot(mix_ref[...], w_ref[...],
                                                    preferred_element_type=jnp.float32)

    if final_norm:
        @pl.when(j == pl.num_programs(1) - 1)
        def _():
            y = y_ref[...]
            ms = jnp.mean(y * y, axis=-1, keepdims=True)
            y_ref[...] = y * lax.rsqrt(ms + EPS) * fw_ref[...]


def _outproj(mix, w_bf16, x2d, final_w, *, tm, tn, final_norm):
    m = x2d.shape[0]
    return pl.pallas_call(
        functools.partial(_outproj_kernel, tn=tn, final_norm=final_norm),
        grid=(m // tm, D_MODEL // tn),
        in_specs=[pl.BlockSpec((tm, D_MODEL), lambda i, j: (i, 0)),
                  pl.BlockSpec((D_MODEL, tn), lambda i, j: (0, j)),
                  pl.BlockSpec((tm, tn), lambda i, j: (i, j)),
                  pl.BlockSpec((1, D_MODEL), lambda i, j: (0, 0))],
        out_specs=pl.BlockSpec((tm, D_MODEL), lambda i, j: (i, 0)),
        out_shape=jax.ShapeDtypeStruct((m, D_MODEL), jnp.float32),
        compiler_params=_params(("parallel", "arbitrary")),
        name="outproj",
    )(mix, w_bf16, x2d, final_w)


def kernel(x_prompt, x_sample, cache_k, cache_v, state_conv, page_table, norm_w, w_in,
           lambda_q1, lambda_k1, lambda_q2, lambda_k2, subln_w, conv_w, w_out, final_norm_w):
    bp, t_p = x_prompt.shape[0], x_prompt.shape[1]
    bs, t_s = x_sample.shape[0], x_sample.shape[1]
    depth = norm_w.shape[0]
    n_pool = cache_k.shape[1]
    xp = x_prompt.reshape(bp * t_p, D_MODEL)
    xs = x_sample.reshape(bs * t_s, D_MODEL)
    tabs_p = _rope_tables(jnp.arange(t_p))
    tabs_p = tuple(jnp.tile(t, (bp, 1)) for t in tabs_p)
    tabs_s = _rope_tables(PAST_LEN + jnp.arange(t_s))
    tabs_s = tuple(jnp.tile(t, (bs, 1)) for t in tabs_s)
    final_w = final_norm_w.reshape(1, D_MODEL)

    kp_l, vp_l, cp_l, ks_l, vs_l, cs_l = [], [], [], [], [], []
    for l in range(depth):
        lam_init = 0.8 - 0.6 * math.exp(-0.3 * l)
        lams = (lambda_q1[l:l + 1], lambda_k1[l:l + 1], lambda_q2[l:l + 1], lambda_k2[l:l + 1])
        w_in_b = w_in[l].astype(jnp.bfloat16)
        w_out_b = w_out[l].astype(jnp.bfloat16)
        nw = norm_w[l:l + 1]
        sw = subln_w[l:l + 1]
        cw = conv_w[l]

        zp = _inproj(xp, nw, w_in_b, tabs_p, tm=512, tn=1024)
        op = _prompt_attn(lams, zp, bp, t_p, tq=512, lam_init=lam_init)
        tm_mix = 256
        mixp, tails = _mix_prompt(op, zp, sw, cw, tm=tm_mix, seq=t_p, lam_init=lam_init)
        xp = _outproj(mixp, w_out_b, xp, final_w, tm=512, tn=1024, final_norm=(l == depth - 1))
        kp_l.append(zp[:, BLK_K * D_ATT:(BLK_K + 1) * D_ATT].reshape(bp, t_p, 2 * N_HEADS, HEAD_QK))
        vp_l.append(zp[:, BLK_V * D_ATT:(BLK_V + 1) * D_ATT].reshape(bp, t_p, N_HEADS, HEAD_V))
        tails = tails.reshape(bp, t_p // tm_mix, SUBLANES, D_CONV)
        cp_l.append(tails[:, -1, SUBLANES - (CONV_W - 1):, :])

        zs = _inproj(xs, nw, w_in_b, tabs_s, tm=512, tn=1024)
        zs3 = zs.reshape(bs, t_s, D_IN)
        cache_kt = jnp.transpose(cache_k[l], (0, 2, 3, 1))
        cache_v2 = cache_v[l].reshape(n_pool, PAGE_SIZE * N_HEADS, HEAD_V)
        os3 = _sample_attn(lams, page_table, zs3, cache_kt, cache_v2, n_pages_step=4, lam_init=lam_init)
        mixs, nst = _mix_sample(os3, zs3, state_conv[l], sw, cw, tb=8, lam_init=lam_init)
        xs = _outproj(mixs.reshape(bs * t_s, 2 * D_ATT), w_out_b, xs, final_w, tm=512, tn=1024,
                      final_norm=(l == depth - 1))
        ks_l.append(zs[:, BLK_K * D_ATT:(BLK_K + 1) * D_ATT].reshape(bs, t_s, 2 * N_HEADS, HEAD_QK))
        vs_l.append(zs[:, BLK_V * D_ATT:(BLK_V + 1) * D_ATT].reshape(bs, t_s, N_HEADS, HEAD_V))
        cs_l.append(nst)

    y_prompt = xp.reshape(bp, t_p, D_MODEL)
    y_sample = xs.reshape(bs, t_s, D_MODEL)
    return (y_prompt, y_sample, jnp.stack(kp_l), jnp.stack(vp_l), jnp.stack(cp_l),
            jnp.stack(ks_l), jnp.stack(vs_l), jnp.stack(cs_l))
```
